```python
import math
import jax, jax.numpy as jnp
from jax import lax
import numpy as np

D_MODEL = 1024
BATCH = 8
SEQ = 8192
DEPTH = 2
DEC_BATCH = 8
DEC_SEQ = 32
PAST_LEN = 1024

CHUNK = 64
N_META = 16
N_HEADS = 8
HEAD_DIM = D_MODEL // 16
D_ATTN = N_HEADS * HEAD_DIM
D_POOL = D_MODEL // 2
POOL_WINDOWS = (2, 4, 8, 16)
N_POOL_GROUPS = len(POOL_WINDOWS)
POOL_GROUP = D_POOL // N_POOL_GROUPS
POOL_STATE = max(POOL_WINDOWS) - 1
D_FF = 4 * D_MODEL
Q_BLOCK = 128
ALPHA = (2.0 * DEPTH) ** 0.25
BETA = (8.0 * DEPTH) ** -0.25
LN_EPS = 1e-5
FORGET_BIAS_INIT = 3.0

OFF_Q = 0
OFF_K = D_ATTN
OFF_V = 2 * D_ATTN
OFF_F = 3 * D_ATTN
OFF_U = OFF_F + N_HEADS
OFF_GA = OFF_U + D_POOL
OFF_GP = OFF_GA + D_MODEL
D_IN = OFF_GP + D_MODEL

kernel_name = "fox_pool_gated_streaming_encoder"


def layer_norm(x, g, b):
    xf = x.astype(jnp.float32)
    mu = jnp.mean(xf, axis=-1, keepdims=True)
    var = jnp.mean(jnp.square(xf - mu), axis=-1, keepdims=True)
    return ((xf - mu) * lax.rsqrt(var + LN_EPS) * g + b).astype(x.dtype)


def in_proj(x, w_in, b_f):
    B, L = x.shape[0], x.shape[1]
    z = jnp.einsum('bld,de->ble', x, w_in)
    q = z[..., OFF_Q:OFF_K].reshape(B, L, N_HEADS, HEAD_DIM)
    k = z[..., OFF_K:OFF_V].reshape(B, L, N_HEADS, HEAD_DIM)
    v = z[..., OFF_V:OFF_F].reshape(B, L, N_HEADS, HEAD_DIM)
    logf = jax.nn.log_sigmoid((z[..., OFF_F:OFF_U] + b_f).astype(jnp.float32))
    u = z[..., OFF_U:OFF_GA]
    ga = z[..., OFF_GA:OFF_GP]
    gp = z[..., OFF_GP:]
    return q, k, v, logf, u, ga, gp


def fox_attend(q, k, v, cq, ck, qpos, kpos):
    s = jnp.einsum('bqhd,bkhd->bhqk', q, k).astype(jnp.float32) / math.sqrt(HEAD_DIM)
    s = s + jnp.transpose(cq, (0, 2, 1))[:, :, :, None] - jnp.transpose(ck, (0, 2, 1))[:, :, None, :]
    mask = kpos[None, :] <= qpos[:, None]
    s = jnp.where(mask[None, None], s, -jnp.inf)
    p = jax.nn.softmax(s, axis=-1).astype(v.dtype)
    return jnp.einsum('bhqk,bkhd->bqhd', p, v)


def fox_prompt(q, k, v, logf):
    B, L = q.shape[0], q.shape[1]
    c = jnp.cumsum(logf, axis=1)
    nb = -(-L // Q_BLOCK)
    pad = nb * Q_BLOCK - L
    qp = jnp.pad(q, ((0, 0), (0, pad), (0, 0), (0, 0)))
    cqp = jnp.pad(c, ((0, 0), (0, pad), (0, 0)))
    qb = qp.reshape(B, nb, Q_BLOCK, N_HEADS, HEAD_DIM).transpose(1, 0, 2, 3, 4)
    cqb = cqp.reshape(B, nb, Q_BLOCK, N_HEADS).transpose(1, 0, 2, 3)
    posb = jnp.arange(nb * Q_BLOCK, dtype=jnp.int32).reshape(nb, Q_BLOCK)
    kpos = jnp.arange(L, dtype=jnp.int32)
    out = lax.map(lambda a: fox_attend(a[0], k, v, a[1], c, a[2], kpos), (qb, cqb, posb))
    out = out.transpose(1, 0, 2, 3, 4).reshape(B, nb * Q_BLOCK, D_ATTN)
    return out[:, :L]


def fox_sample(q, k, v, logf, ck, cv, clf):
    B, T = q.shape[0], q.shape[1]
    P = ck.shape[1]
    kf = jnp.concatenate([ck, k], axis=1)
    vf = jnp.concatenate([cv, v], axis=1)
    c = jnp.cumsum(jnp.concatenate([clf.astype(jnp.float32), logf], axis=1), axis=1)
    qpos = P + jnp.arange(T, dtype=jnp.int32)
    kpos = jnp.arange(P + T, dtype=jnp.int32)
    out = fox_attend(q, kf, vf, c[:, P:], c, qpos, kpos)
    return out.reshape(B, T, D_ATTN)


def pool_branch(u_ext, pos, w_grp, scale):
    B = u_ext.shape[0]
    n = u_ext.shape[1] - POOL_STATE
    uf = u_ext.astype(jnp.float32)
    cs = jnp.pad(jnp.cumsum(uf, axis=1), ((0, 0), (1, 0), (0, 0)))
    cur = uf[:, POOL_STATE:]
    outs = []
    for g, w in enumerate(POOL_WINDOWS):
        sl = slice(g * POOL_GROUP, (g + 1) * POOL_GROUP)
        end = cs[:, POOL_STATE + 1:POOL_STATE + 1 + n, sl]
        start = cs[:, POOL_STATE + 1 - w:POOL_STATE + 1 - w + n, sl]
        cnt = jnp.minimum(w, pos + 1).astype(jnp.float32)[None, :, None]
        outs.append((end - start) / cnt - cur[..., sl])
    pooled = jnp.stack(outs, axis=2)
    mixed = jnp.einsum('bngc,gce->bnge', pooled.astype(w_grp.dtype), w_grp).reshape(B, n, D_POOL)
    return mixed * scale


def merge(att, pooled, ga, gp, w_attn_br, w_pool_br, w_out):
    a = jnp.einsum('bla,ad->bld', att, w_attn_br)
    p = jnp.einsum('blc,cd->bld', pooled, w_pool_br)
    m = jax.nn.sigmoid(ga) * a + jax.nn.sigmoid(gp) * p
    return jnp.einsum('bld,de->ble', m, w_out)


def sqrelu_mlp(x, w_up, w_down):
    h = jax.nn.relu(jnp.einsum('bld,df->blf', x, w_up))
    return jnp.einsum('blf,fd->bld', h * h, w_down)


def setup_inputs(seed: int = 0) -> dict:
    key = jax.random.key(seed)
    ks = jax.random.split(key, 24)
    f32 = jnp.float32
    col_scale = jnp.ones((D_IN,), f32).at[OFF_V:OFF_F].set(BETA)
    return {
        "x_prompt": jax.random.normal(ks[0], (BATCH, SEQ, D_MODEL), f32),
        "x_sample": jax.random.normal(ks[1], (DEC_BATCH, DEC_SEQ, D_MODEL), f32),
        "cache_k": jax.random.normal(ks[2], (DEPTH, DEC_BATCH, PAST_LEN, N_HEADS, HEAD_DIM), f32),
        "cache_v": jax.random.normal(ks[3], (DEPTH, DEC_BATCH, PAST_LEN, N_HEADS, HEAD_DIM), f32) * BETA,
        "cache_logf": jax.nn.log_sigmoid(FORGET_BIAS_INIT + jax.random.normal(ks[4], (DEPTH, DEC_BATCH, PAST_LEN, N_HEADS), f32)),
        "state_pool": jax.random.normal(ks[5], (DEPTH, DEC_BATCH, POOL_STATE, D_POOL), f32),
        "meta_tokens": jax.random.normal(ks[6], (N_META, D_MODEL), f32),
        "w_in": jax.random.normal(ks[7], (DEPTH, D_MODEL, D_IN), f32) * (D_MODEL ** -0.5) * col_scale,
        "b_f": FORGET_BIAS_INIT + 0.1 * jax.random.normal(ks[8], (DEPTH, N_HEADS), f32),
        "w_pool_grp": jax.random.normal(ks[9], (DEPTH, N_POOL_GROUPS, POOL_GROUP, POOL_GROUP), f32) * (POOL_GROUP ** -0.5),
        "pool_scale": 1.0 + 0.1 * jax.random.normal(ks[10], (DEPTH, D_POOL), f32),
        "w_attn_br": jax.random.normal(ks[11], (DEPTH, D_ATTN, D_MODEL), f32) * (D_ATTN ** -0.5),
        "w_pool_br": jax.random.normal(ks[12], (DEPTH, D_POOL, D_MODEL), f32) * (D_POOL ** -0.5),
        "w_out": jax.random.normal(ks[13], (DEPTH, D_MODEL, D_MODEL), f32) * (D_MODEL ** -0.5) * BETA,
        "ln1_g": 1.0 + 0.02 * jax.random.normal(ks[14], (DEPTH, D_MODEL), f32),
        "ln1_b": 0.02 * jax.random.normal(ks[15], (DEPTH, D_MODEL), f32),
        "w_up": jax.random.normal(ks[16], (DEPTH, D_MODEL, D_FF), f32) * (D_MODEL ** -0.5) * BETA,
        "w_down": jax.random.normal(ks[17], (DEPTH, D_FF, D_MODEL), f32) * (D_FF ** -0.5) * BETA,
        "ln2_g": 1.0 + 0.02 * jax.random.normal(ks[18], (DEPTH, D_MODEL), f32),
        "ln2_b": 0.02 * jax.random.normal(ks[19], (DEPTH, D_MODEL), f32),
    }


def reference(x_prompt, x_sample, cache_k, cache_v, cache_logf, state_pool, meta_tokens,
              w_in, b_f, w_pool_grp, pool_scale, w_attn_br, w_pool_br, w_out,
              ln1_g, ln1_b, w_up, w_down, ln2_g, ln2_b):
    B = x_prompt.shape[0]
    meta = jnp.broadcast_to(meta_tokens.astype(x_prompt.dtype)[None], (B, N_META, D_MODEL))
    xp = jnp.concatenate([meta, x_prompt], axis=1)
    xs = x_sample
    L = xp.shape[1]
    T = xs.shape[1]
    pos_p = jnp.arange(L, dtype=jnp.int32)
    pos_s = PAST_LEN + jnp.arange(T, dtype=jnp.int32)
    kp_l, vp_l, fp_l, pp_l = [], [], [], []
    ks_l, vs_l, fs_l, ps_l = [], [], [], []
    for l in range(DEPTH):
        q, k, v, lf, u, ga, gp = in_proj(xp, w_in[l], b_f[l])
        att = fox_prompt(q, k, v, lf)
        u_ext = jnp.concatenate([jnp.zeros((B, POOL_STATE, D_POOL), u.dtype), u], axis=1)
        pooled = pool_branch(u_ext, pos_p, w_pool_grp[l], pool_scale[l])
        mix = merge(att, pooled, ga, gp, w_attn_br[l], w_pool_br[l], w_out[l])
        xp = layer_norm(ALPHA * xp + mix, ln1_g[l], ln1_b[l])
        xp = layer_norm(ALPHA * xp + sqrelu_mlp(xp, w_up[l], w_down[l]), ln2_g[l], ln2_b[l])
        kp_l.append(k)
        vp_l.append(v)
        fp_l.append(lf)
        pp_l.append(u_ext[:, -POOL_STATE:])
        q, k, v, lf, u, ga, gp = in_proj(xs, w_in[l], b_f[l])
        att = fox_sample(q, k, v, lf, cache_k[l], cache_v[l], cache_logf[l])
        u_ext = jnp.concatenate([state_pool[l].astype(u.dtype), u], axis=1)
        pooled = pool_branch(u_ext, pos_s, w_pool_grp[l], pool_scale[l])
        mix = merge(att, pooled, ga, gp, w_attn_br[l], w_pool_br[l], w_out[l])
        xs = layer_norm(ALPHA * xs + mix, ln1_g[l], ln1_b[l])
        xs = layer_norm(ALPHA * xs + sqrelu_mlp(xs, w_up[l], w_down[l]), ln2_g[l], ln2_b[l])
        ks_l.append(k)
        vs_l.append(v)
        fs_l.append(lf)
        ps_l.append(u_ext[:, -POOL_STATE:])
    y_prompt = xp[:, N_META:]
    y_sample = xs
    k_prompt = jnp.stack(kp_l)
    v_prompt = jnp.stack(vp_l)
    logf_prompt = jnp.stack(fp_l)
    pool_prompt = jnp.stack(pp_l)
    k_sample = jnp.stack(ks_l)
    v_sample = jnp.stack(vs_l)
    logf_sample = jnp.stack(fs_l)
    pool_sample = jnp.stack(ps_l)
    return (y_prompt, y_sample, k_prompt, v_prompt, logf_prompt, pool_prompt, k_sample, v_sample, logf_sample, pool_sample)
```

```python
import functools
import math

import jax
import jax.numpy as jnp
from jax import lax
from jax.experimental import pallas as pl
from jax.experimental.pallas import tpu as pltpu

F32 = jnp.float32
BF16 = jnp.bfloat16

N_HEADS = 8
HEAD_DIM = 64
PAIR = 2 * HEAD_DIM
N_PAIRS = N_HEADS // 2
POOL_WINDOWS = (2, 4, 8, 16)
POOL_GROUP = 128
HALO = 16
POOL_STATE = 15
LN_EPS = 1e-5
NEG = -1e30
LANES = 128
VMEM_LIMIT = 56 * 1024 * 1024


def _dot(a, b):
    return jnp.dot(a, b, preferred_element_type=F32)


def _dot_nt(a, b):
    return lax.dot_general(a, b, (((1,), (1,)), ((), ())), preferred_element_type=F32)


def _tile(n, pref, mult=16):
    best = None
    for t in range(mult, min(n, pref) + 1, mult):
        if n % t == 0:
            best = t
    return best if best is not None else n


def _params(sem):
    return pltpu.CompilerParams(dimension_semantics=sem, vmem_limit_bytes=VMEM_LIMIT)


def _const_spec(shape):
    nd = len(shape)
    return pl.BlockSpec(shape, lambda *_: (0,) * nd, pipeline_mode=pl.Buffered(1))


D_ATTN = N_HEADS * HEAD_DIM
D_POOL = 4 * POOL_GROUP


def _in_proj_kernel(x_ref, w_ref, bf_ref, q_ref, k_ref, v_ref, kb_ref, vb_ref, u_ref, g_ref, lf_ref):
    d_model = x_ref.shape[1]
    x = x_ref[...].astype(BF16)
    o = 0
    q_ref[...] = (_dot(x, w_ref[:, o:o + D_ATTN]) * (1.0 / math.sqrt(HEAD_DIM))).astype(BF16)
    o += D_ATTN
    k = _dot(x, w_ref[:, o:o + D_ATTN])
    k_ref[...] = k
    kb_ref[...] = k.astype(BF16)
    o += D_ATTN
    v = _dot(x, w_ref[:, o:o + D_ATTN])
    v_ref[...] = v
    vb_ref[...] = v.astype(BF16)
    o += D_ATTN
    u_ref[...] = _dot(x, w_ref[:, o:o + D_POOL])
    o += D_POOL
    for c in range(2):
        g_ref[:, c * d_model:(c + 1) * d_model] = _dot(x, w_ref[:, o:o + d_model]).astype(BF16)
        o += d_model
    z = _dot(x, w_ref[:, o:o + LANES]) + bf_ref[...]
    lf = jnp.minimum(z, 0.0) - jnp.log1p(jnp.exp(-jnp.abs(z)))
    lf_ref[...] = lf[:, :N_HEADS]


def _in_proj(x, w_cat, b_f):
    rows, d_model = x.shape
    tm = _tile(rows, 512)
    n_cols = w_cat.shape[1]
    row = lambda i: (i, 0)
    outs = [
        jax.ShapeDtypeStruct((rows, D_ATTN), BF16),
        jax.ShapeDtypeStruct((rows, D_ATTN), F32),
        jax.ShapeDtypeStruct((rows, D_ATTN), F32),
        jax.ShapeDtypeStruct((rows, D_ATTN), BF16),
        jax.ShapeDtypeStruct((rows, D_ATTN), BF16),
        jax.ShapeDtypeStruct((rows, D_POOL), F32),
        jax.ShapeDtypeStruct((rows, 2 * d_model), BF16),
        jax.ShapeDtypeStruct((rows, N_HEADS), F32),
    ]
    return pl.pallas_call(
        _in_proj_kernel,
        grid=(rows // tm,),
        in_specs=[pl.BlockSpec((tm, d_model), row), _const_spec((d_model, n_cols)), _const_spec((1, LANES))],
        out_specs=[pl.BlockSpec((tm, s.shape[1]), row) for s in outs],
        out_shape=outs,
        compiler_params=_params(("parallel",)),
        name="in_proj",
    )(x, w_cat, b_f)


def _cumsum_kernel(x_ref, o_ref):
    x = x_ref[...]
    n = x.shape[1]
    idx = lax.broadcasted_iota(jnp.int32, x.shape, 1)
    s = 1
    while s < n:
        x = x + jnp.where(idx >= s, pltpu.roll(x, s, 1), 0.0)
        s *= 2
    o_ref[...] = x


def _cumsum(x):
    b, h, n = x.shape
    spec = pl.BlockSpec((None, h, n), lambda i: (i, 0, 0))
    return pl.pallas_call(
        _cumsum_kernel, grid=(b,), in_specs=[spec], out_specs=spec,
        out_shape=jax.ShapeDtypeStruct(x.shape, F32),
        compiler_params=_params(("parallel",)), name="logf_cumsum",
    )(x)


def _softmax_step(s, v_blk, m_ref, l_ref, acc_ref):
    tk = s.shape[1]
    m_prev = m_ref[...]
    m_cur = s[:, :LANES]
    for c in range(1, tk // LANES):
        m_cur = jnp.maximum(m_cur, s[:, c * LANES:(c + 1) * LANES])
    m_next = jnp.maximum(m_prev, jnp.max(m_cur, axis=1, keepdims=True))
    p = jnp.exp(s - pltpu.repeat(m_next, tk // LANES, axis=1))
    alpha = jnp.exp(m_prev - m_next)
    l_ref[...] = alpha * l_ref[...] + jnp.sum(p, axis=1, keepdims=True)
    acc_ref[...] = alpha * acc_ref[...] + _dot(p.astype(BF16), v_blk)
    m_ref[...] = m_next


def _attn_kernel(q_ref, k_ref, v_ref, c_ref, km_ref, vm_ref, cm_ref, o_ref, m_sc, l_sc, acc_sc, *, tq):
    i = pl.program_id(2)
    q0 = pl.multiple_of(i * tq, tq)
    lane = lax.broadcasted_iota(jnp.int32, (tq, PAIR), 1)
    q = q_ref[...]
    zero = jnp.zeros_like(q)
    row = lax.broadcasted_iota(jnp.int32, (tq, tq), 0)
    col = lax.broadcasted_iota(jnp.int32, (tq, tq), 1)
    mlane = lax.broadcasted_iota(jnp.int32, (tq, LANES), 1)
    for hh in range(2):
        qh = jnp.where((lane >= HEAD_DIM) == (hh == 1), q, zero)
        m_ref, l_ref, acc_ref = m_sc.at[hh], l_sc.at[hh], acc_sc.at[hh]
        c0 = c_ref[hh:hh + 1, pl.ds(q0, LANES)][:, 0:1]

        cm = cm_ref[hh:hh + 1, :]
        cm_key = cm - cm[:, HALO - 1:HALO]
        s = _dot_nt(qh, km_ref[...]) + (c0 - cm_key)
        s = jnp.where(mlane < HALO, s, NEG)
        m0 = jnp.max(s, axis=1, keepdims=True)
        p = jnp.exp(s - m0)
        m_ref[...] = jnp.broadcast_to(m0, (tq, LANES))
        l_ref[...] = jnp.broadcast_to(jnp.sum(p, axis=1, keepdims=True), (tq, LANES))
        acc_ref[...] = _dot(p.astype(BF16), vm_ref[...])

        def body(j, carry):
            k0 = pl.multiple_of(j * tq, tq)
            s = _dot_nt(qh, k_ref[pl.ds(k0, tq), :]) + (c0 - c_ref[hh:hh + 1, pl.ds(k0, tq)])
            _softmax_step(s, v_ref[pl.ds(k0, tq), :], m_ref, l_ref, acc_ref)
            return carry

        lax.fori_loop(0, i, body, 0)

        s = _dot_nt(qh, k_ref[pl.ds(q0, tq), :]) + (c0 - c_ref[hh:hh + 1, pl.ds(q0, tq)])
        s = jnp.where(col <= row, s, NEG)
        _softmax_step(s, v_ref[pl.ds(q0, tq), :], m_ref, l_ref, acc_ref)

    o = jnp.where(lane < HEAD_DIM, acc_sc[0] / l_sc[0], acc_sc[1] / l_sc[1])
    o_ref[...] = o.astype(o_ref.dtype)


def _attn(q, kb, vb, c, km, vm, cm):
    b, s, _ = q.shape
    tq = _tile(s, 512, 128)
    kernel = functools.partial(_attn_kernel, tq=tq)
    return pl.pallas_call(
        kernel,
        grid=(b, N_PAIRS, s // tq),
        in_specs=[
            pl.BlockSpec((None, tq, PAIR), lambda bi, p, i: (bi, i, p)),
            pl.BlockSpec((None, s, PAIR), lambda bi, p, i: (bi, 0, p)),
            pl.BlockSpec((None, s, PAIR), lambda bi, p, i: (bi, 0, p)),
            pl.BlockSpec((None, None, 2, s), lambda bi, p, i: (bi, p, 0, 0)),
            pl.BlockSpec((LANES, PAIR), lambda bi, p, i: (0, p)),
            pl.BlockSpec((LANES, PAIR), lambda bi, p, i: (0, p)),
            pl.BlockSpec((None, 2, LANES), lambda bi, p, i: (p, 0, 0)),
        ],
        out_specs=pl.BlockSpec((None, tq, PAIR), lambda bi, p, i: (bi, i, p)),
        out_shape=jax.ShapeDtypeStruct(q.shape, BF16),
        scratch_shapes=[pltpu.VMEM((2, tq, LANES), F32)] * 3,
        compiler_params=_params(("parallel", "parallel", "arbitrary")),
        name="fox_attn",
    )(q, kb, vb, c, km, vm, cm)


def _attn_small_kernel(q_ref, k_ref, v_ref, c_ref, o_ref, *, past):
    t = q_ref.shape[0]
    lk = k_ref.shape[0]
    lane = lax.broadcasted_iota(jnp.int32, (t, PAIR), 1)
    q = q_ref[...]
    zero = jnp.zeros_like(q)
    qpos = past + lax.broadcasted_iota(jnp.int32, (t, lk), 0)
    kpos = lax.broadcasted_iota(jnp.int32, (t, lk), 1)
    outs = []
    for hh in range(2):
        qh = jnp.where((lane >= HEAD_DIM) == (hh == 1), q, zero)
        c = c_ref[hh:hh + 1, :]
        s = _dot_nt(qh, k_ref[...]) + (c[:, past:past + 1] - c)
        s = jnp.where(kpos <= qpos, s, NEG)
        p = jnp.exp(s - jnp.max(s, axis=1, keepdims=True))
        l = jnp.sum(p, axis=1, keepdims=True)
        outs.append(_dot(p.astype(BF16), v_ref[...]) / l)
    o_ref[...] = jnp.where(lane < HEAD_DIM, outs[0], outs[1]).astype(o_ref.dtype)


def _attn_small(q, kb, vb, c, past):
    b, t, _ = q.shape
    lk = kb.shape[1]
    kernel = functools.partial(_attn_small_kernel, past=past)
    return pl.pallas_call(
        kernel,
        grid=(b, N_PAIRS),
        in_specs=[
            pl.BlockSpec((None, t, PAIR), lambda bi, p: (bi, 0, p)),
            pl.BlockSpec((None, lk, PAIR), lambda bi, p: (bi, 0, p)),
            pl.BlockSpec((None, lk, PAIR), lambda bi, p: (bi, 0, p)),
            pl.BlockSpec((None, None, 2, lk), lambda bi, p: (bi, p, 0, 0)),
        ],
        out_specs=pl.BlockSpec((None, t, PAIR), lambda bi, p: (bi, 0, p)),
        out_shape=jax.ShapeDtypeStruct(q.shape, BF16),
        compiler_params=_params(("parallel", "parallel")),
        name="fox_attn_small",
    )(q, kb, vb, c)


def _layer_norm(x, g, b):
    mu = jnp.mean(x, axis=-1, keepdims=True)
    xc = x - mu
    var = jnp.mean(xc * xc, axis=-1, keepdims=True)
    return xc * lax.rsqrt(var + LN_EPS) * g + b


def _post_kernel(x_ref, att_ref, u_ref, uprev_ref, halo0_ref, g_ref,
                 wgrp_ref, scale_ref, wab_ref, wpb_ref, wout_ref, ln1g_ref, ln1b_ref,
                 wup_ref, wdn_ref, ln2g_ref, ln2b_ref, y_ref, ue_sc, pool_sc, acc_sc,
                 *, alpha, pos0, ff_chunk):
    bb, tm, d_model = x_ref.shape
    rows = bb * tm
    i = pl.program_id(1)

    first = i == 0
    for bi in range(bb):
        ue_sc[bi, 0:HALO, :] = jnp.where(first, halo0_ref[bi], uprev_ref[bi])
        ue_sc[bi, HALO:HALO + tm, :] = u_ref[bi]
    pos = pos0 + i * tm + lax.broadcasted_iota(jnp.int32, (tm, 1), 0)
    for gi, w in enumerate(POOL_WINDOWS):
        ls = slice(gi * POOL_GROUP, (gi + 1) * POOL_GROUP)
        inv_cnt = 1.0 / jnp.minimum(w, pos + 1).astype(F32)
        for bi in range(bb):
            cur = ue_sc[bi, HALO:HALO + tm, ls]
            tot = cur
            for j in range(1, w):
                tot = tot + ue_sc[bi, HALO - j:HALO - j + tm, ls]
            pool_sc[bi * tm:(bi + 1) * tm, ls] = tot * inv_cnt - cur

    mixed = [_dot(pool_sc[:, gi * POOL_GROUP:(gi + 1) * POOL_GROUP].astype(BF16), wgrp_ref[gi])
             for gi in range(len(POOL_WINDOWS))]
    mixed = jnp.concatenate(mixed, axis=1) * scale_ref[...]
    p = _dot(mixed.astype(BF16), wpb_ref[...])
    a = _dot(att_ref[...].reshape(rows, att_ref.shape[2]), wab_ref[...])
    g = g_ref[...].reshape(rows, 2 * d_model).astype(F32)
    m = jax.nn.sigmoid(g[:, :d_model]) * a + jax.nn.sigmoid(g[:, d_model:]) * p
    mix = _dot(m.astype(BF16), wout_ref[...])
    x = x_ref[...].reshape(rows, d_model)
    x1 = _layer_norm(alpha * x + mix, ln1g_ref[...], ln1b_ref[...])

    x1b = x1.astype(BF16)
    d_ff = wup_ref.shape[1]
    for ci, o in enumerate(range(0, d_ff, ff_chunk)):
        h = jnp.maximum(_dot(x1b, wup_ref[:, o:o + ff_chunk]), 0.0)
        part = _dot((h * h).astype(BF16), wdn_ref[o:o + ff_chunk, :])
        if ci == 0:
            acc_sc[...] = part
        else:
            acc_sc[...] += part
    y = _layer_norm(alpha * x1 + acc_sc[...], ln2g_ref[...], ln2b_ref[...])
    y_ref[...] = y.reshape(bb, tm, d_model)


def _post(x, att, u, halo0, g, w, *, alpha, pos0, bb, tm):
    b, t, d_model = x.shape
    nb, nt = b // bb, t // tm
    hb = tm // HALO
    blk = lambda c: pl.BlockSpec((bb, tm, c), lambda bi, i: (bi, i, 0))
    halo0_map = (lambda bi, i: (bi, 0, 0)) if halo0.shape[0] == b else (lambda bi, i: (0, 0, 0))
    if halo0.shape[0] != b:
        assert bb == 1
    weights = [w["w_grp"], w["scale"], w["w_ab"], w["w_pb"], w["w_out"], w["ln1_g"], w["ln1_b"],
               w["w_up"], w["w_dn"], w["ln2_g"], w["ln2_b"]]
    kernel = functools.partial(_post_kernel, alpha=alpha, pos0=pos0, ff_chunk=min(1024, w["w_up"].shape[1]))
    return pl.pallas_call(
        kernel,
        grid=(nb, nt),
        in_specs=[
            blk(d_model), blk(att.shape[2]), blk(u.shape[2]),
            pl.BlockSpec((bb, HALO, u.shape[2]), lambda bi, i: (bi, jnp.maximum(i * hb - 1, 0), 0)),
            pl.BlockSpec((bb, HALO, u.shape[2]), halo0_map),
            blk(2 * d_model),
        ] + [_const_spec(a.shape) for a in weights],
        out_specs=blk(d_model),
        out_shape=jax.ShapeDtypeStruct(x.shape, F32),
        scratch_shapes=[
            pltpu.VMEM((bb, HALO + tm, u.shape[2]), F32),
            pltpu.VMEM((bb * tm, u.shape[2]), F32),
            pltpu.VMEM((bb * tm, d_model), F32),
        ],
        compiler_params=_params(("parallel", "arbitrary")),
        name="post_mix_mlp",
    )(x, att, u, u, halo0, g, *weights)


def _layer_weights(l, w_in, b_f, w_pool_grp, pool_scale, w_attn_br, w_pool_br, w_out,
                   ln1_g, ln1_b, w_up, w_down, ln2_g, ln2_b):
    d_model = w_in.shape[1]
    wi = w_in[l]
    o_f = 3 * D_ATTN
    o_u = o_f + N_HEADS
    pad = jnp.zeros((d_model, LANES - N_HEADS), wi.dtype)
    w_cat = jnp.concatenate([wi[:, :o_f], wi[:, o_u:], wi[:, o_f:o_u], pad], axis=1).astype(BF16)
    bf = jnp.zeros((1, LANES), F32).at[0, :N_HEADS].set(b_f[l].astype(F32))
    row = lambda a: a[l].astype(F32).reshape(1, -1)
    return dict(
        w_cat=w_cat, b_f=bf,
        w_grp=w_pool_grp[l].astype(BF16), scale=row(pool_scale),
        w_ab=w_attn_br[l].astype(BF16), w_pb=w_pool_br[l].astype(BF16), w_out=w_out[l].astype(BF16),
        ln1_g=row(ln1_g), ln1_b=row(ln1_b), w_up=w_up[l].astype(BF16), w_dn=w_down[l].astype(BF16),
        ln2_g=row(ln2_g), ln2_b=row(ln2_b),
    )


def _pad_rows(a, n):
    return jnp.pad(a, ((0, 0), (0, n - a.shape[1]), (0, 0)))


def _c_rows(lf, n):
    b = lf.shape[0]
    x = jnp.swapaxes(_pad_rows(lf, n), 1, 2)
    return _cumsum(x).reshape(b, N_PAIRS, 2, n)


def _round_up(n, m):
    return -(-n // m) * m


def kernel(x_prompt, x_sample, cache_k, cache_v, cache_logf, state_pool, meta_tokens, w_in, b_f, w_pool_grp,
           pool_scale, w_attn_br, w_pool_br, w_out, ln1_g, ln1_b, w_up, w_down, ln2_g, ln2_b):
    depth = w_in.shape[0]
    b, seq, d_model = x_prompt.shape
    bs, t_dec, _ = x_sample.shape
    past = cache_k.shape[2]
    n_meta = meta_tokens.shape[0]
    assert n_meta == HALO and t_dec >= POOL_STATE and seq >= POOL_STATE
    assert cache_k.shape[3] == N_HEADS and cache_k.shape[4] == HEAD_DIM
    alpha = (2.0 * depth) ** 0.25
    lk_dec = _round_up(past + t_dec, LANES)

    xm = meta_tokens.astype(F32)[None]
    xr = x_prompt
    xs = x_sample
    tm_r = _tile(seq, 512)
    outs = {n: [] for n in ("kp", "vp", "fp", "pp", "ks", "vs", "fs", "ps")}
    for l in range(depth):
        w = _layer_weights(l, w_in, b_f, w_pool_grp, pool_scale, w_attn_br, w_pool_br, w_out,
                           ln1_g, ln1_b, w_up, w_down, ln2_g, ln2_b)

        qm, km, vm, kbm, vbm, um, gm, lfm = _in_proj(xm[0], w["w_cat"], w["b_f"])
        cm = _c_rows(lfm[None], LANES)
        kbm_p, vbm_p = _pad_rows(kbm[None], LANES), _pad_rows(vbm[None], LANES)
        attm = _attn_small(qm[None], kbm_p, vbm_p, cm, 0)
        xm = _post(xm, attm, um[None], jnp.zeros((1, HALO, D_POOL), F32), gm[None], w,
                   alpha=alpha, pos0=0, bb=1, tm=n_meta)

        q, k, v, kb, vb, u, g, lf = _in_proj(xr.reshape(b * seq, d_model), w["w_cat"], w["b_f"])
        r3 = lambda a: a.reshape(b, seq, a.shape[-1])
        c = _c_rows(r3(lf), seq)
        att = _attn(r3(q), r3(kb), r3(vb), c, kbm_p[0], vbm_p[0], cm[0])
        xr = _post(xr, att, r3(u), um[None], r3(g), w, alpha=alpha, pos0=n_meta, bb=1, tm=tm_r)
        bc = lambda a: jnp.broadcast_to(a[None], (b,) + a.shape)
        outs["kp"].append(jnp.concatenate([bc(km), r3(k)], axis=1).reshape(b, n_meta + seq, N_HEADS, HEAD_DIM))
        outs["vp"].append(jnp.concatenate([bc(vm), r3(v)], axis=1).reshape(b, n_meta + seq, N_HEADS, HEAD_DIM))
        outs["fp"].append(jnp.concatenate([bc(lfm), r3(lf)], axis=1))
        outs["pp"].append(r3(u)[:, seq - POOL_STATE:])

        qs, ks, vs, kbs, vbs, us, gs, lfs = _in_proj(xs.reshape(bs * t_dec, d_model), w["w_cat"], w["b_f"])
        s3 = lambda a: a.reshape(bs, t_dec, a.shape[-1])
        ck = cache_k[l].reshape(bs, past, D_ATTN).astype(BF16)
        cv = cache_v[l].reshape(bs, past, D_ATTN).astype(BF16)
        kfull = _pad_rows(jnp.concatenate([ck, s3(kbs)], axis=1), lk_dec)
        vfull = _pad_rows(jnp.concatenate([cv, s3(vbs)], axis=1), lk_dec)
        cs = _c_rows(jnp.concatenate([cache_logf[l].astype(F32), s3(lfs)], axis=1), lk_dec)
        atts = _attn_small(s3(qs), kfull, vfull, cs, past)
        hist = jnp.concatenate([jnp.zeros((bs, HALO - POOL_STATE, D_POOL), F32), state_pool[l].astype(F32)], axis=1)
        xs = _post(xs, atts, s3(us), hist, s3(gs), w, alpha=alpha, pos0=past, bb=bs, tm=t_dec)
        outs["ks"].append(s3(ks).reshape(bs, t_dec, N_HEADS, HEAD_DIM))
        outs["vs"].append(s3(vs).reshape(bs, t_dec, N_HEADS, HEAD_DIM))
        outs["fs"].append(s3(lfs))
        outs["ps"].append(s3(us)[:, t_dec - POOL_STATE:])

    st = {n: jnp.stack(v) for n, v in outs.items()}
    return (xr, xs, st["kp"], st["vp"], st["fp"], st["pp"], st["ks"], st["vs"], st["fs"], st["ps"])
```

```python
import functools
import math

import jax
import jax.numpy as jnp
from jax import lax
from jax.experimental import pallas as pl
from jax.experimental.pallas import tpu as pltpu

F32 = jnp.float32
BF16 = jnp.bfloat16

N_HEADS = 8
HEAD_DIM = 64
PAIR = 2 * HEAD_DIM
N_PAIRS = N_HEADS // 2
POOL_WINDOWS = (2, 4, 8, 16)
POOL_GROUP = 128
HALO = 16
POOL_STATE = 15
LN_EPS = 1e-5
NEG = -1e30
LANES = 128
SUBLANES = 8
VMEM_LIMIT = 56 * 1024 * 1024


def _dot(a, b):
    return jnp.dot(a, b, preferred_element_type=F32)


def _dot_nt(a, b):
    return lax.dot_general(a, b, (((1,), (1,)), ((), ())), preferred_element_type=F32)


def _tile(n, pref, mult=16):
    best = None
    for t in range(mult, min(n, pref) + 1, mult):
        if n % t == 0:
            best = t
    return best if best is not None else n


def _params(sem):
    return pltpu.CompilerParams(dimension_semantics=sem, vmem_limit_bytes=VMEM_LIMIT)


def _const_spec(shape):
    nd = len(shape)
    return pl.BlockSpec(shape, lambda *_: (0,) * nd, pipeline_mode=pl.Buffered(1))


D_ATTN = N_HEADS * HEAD_DIM
D_POOL = 4 * POOL_GROUP
LOG2E = math.log2(math.e)
Q_SCALE = LOG2E / math.sqrt(HEAD_DIM)


def _in_proj_kernel(x_ref, w_ref, bf_ref, q_ref, k_ref, v_ref, kb_ref, vb_ref, u_ref, g_ref, lf_ref):
    d_model = x_ref.shape[1]
    x = x_ref[...].astype(BF16)
    o = 0
    q_ref[...] = (_dot(x, w_ref[:, o:o + D_ATTN]) * Q_SCALE).astype(BF16)
    o += D_ATTN
    k = _dot(x, w_ref[:, o:o + D_ATTN])
    k_ref[...] = k
    kb_ref[...] = k.astype(BF16)
    o += D_ATTN
    v = _dot(x, w_ref[:, o:o + D_ATTN])
    v_ref[...] = v
    vb_ref[...] = v.astype(BF16)
    o += D_ATTN
    u_ref[...] = _dot(x, w_ref[:, o:o + D_POOL])
    o += D_POOL
    for c in range(2):
        g_ref[:, c * d_model:(c + 1) * d_model] = _dot(x, w_ref[:, o:o + d_model]).astype(BF16)
        o += d_model
    z = _dot(x, w_ref[:, o:o + LANES]) + bf_ref[...]
    lf = jnp.minimum(z, 0.0) - jnp.log1p(jnp.exp(-jnp.abs(z)))
    lf_ref[...] = lf[:, :N_HEADS]


def _in_proj(x, w_cat, b_f):
    rows, d_model = x.shape
    tm = _tile(rows, 512)
    n_cols = w_cat.shape[1]
    row = lambda i: (i, 0)
    outs = [
        jax.ShapeDtypeStruct((rows, D_ATTN), BF16),
        jax.ShapeDtypeStruct((rows, D_ATTN), F32),
        jax.ShapeDtypeStruct((rows, D_ATTN), F32),
        jax.ShapeDtypeStruct((rows, D_ATTN), BF16),
        jax.ShapeDtypeStruct((rows, D_ATTN), BF16),
        jax.ShapeDtypeStruct((rows, D_POOL), F32),
        jax.ShapeDtypeStruct((rows, 2 * d_model), BF16),
        jax.ShapeDtypeStruct((rows, N_HEADS), F32),
    ]
    return pl.pallas_call(
        _in_proj_kernel,
        grid=(rows // tm,),
        in_specs=[pl.BlockSpec((tm, d_model), row), _const_spec((d_model, n_cols)), _const_spec((1, LANES))],
        out_specs=[pl.BlockSpec((tm, s.shape[1]), row) for s in outs],
        out_shape=outs,
        compiler_params=_params(("parallel",)),
        name="in_proj",
    )(x, w_cat, b_f)


def _cumsum_kernel(x_ref, c_ref, parts_ref, *, origin, n_valid):
    x = x_ref[...]
    n = x.shape[1]
    idx = lax.broadcasted_iota(jnp.int32, x.shape, 1)
    s = 1
    while s < n:
        x = x + jnp.where(idx >= s, pltpu.roll(x, s, 1), 0.0)
        s *= 2
    c = x * LOG2E
    c_ref[...] = c
    bias = -c if origin is None else c[:, origin:origin + 1] - c
    if n_valid < n:
        bias = jnp.where(idx < n_valid, bias, NEG)
    hi = bias.astype(BF16).astype(F32)
    r = bias - hi
    mid = r.astype(BF16).astype(F32)
    parts_ref[0] = hi
    parts_ref[1] = mid
    parts_ref[2] = (r - mid).astype(BF16).astype(F32)


def _cumsum(x, origin=None, n_valid=None):
    b, h, n = x.shape
    spec = pl.BlockSpec((None, h, n), lambda i: (i, 0, 0))
    kernel = functools.partial(_cumsum_kernel, origin=origin, n_valid=n if n_valid is None else n_valid)
    return pl.pallas_call(
        kernel, grid=(b,), in_specs=[spec],
        out_specs=[spec, pl.BlockSpec((None, 3, h, n), lambda i: (i, 0, 0, 0))],
        out_shape=[jax.ShapeDtypeStruct(x.shape, F32), jax.ShapeDtypeStruct((b, 3, h, n), F32)],
        compiler_params=_params(("parallel",)), name="logf_cumsum",
    )(x)


ATTN_TK = 512
KA = 2 * LANES
V_ROWS = PAIR + 16
MXU_N = 256


def _softmax_cols(s_ref, p_ref, a_ref, m_ref, jq, nkeys, first=False, key_off=None, mx_ref=None):
    cols = slice(jq * LANES, (jq + 1) * LANES)
    nkb = nkeys // LANES
    if key_off is not None:
        tri = (lax.broadcasted_iota(jnp.int32, (LANES, LANES), 0)
               <= lax.broadcasted_iota(jnp.int32, (LANES, LANES), 1))

    def load(kb):
        if key_off is not None:
            k_lo = key_off + kb * LANES
            if k_lo > jq * LANES:
                return None
        blk = s_ref[kb * LANES:(kb + 1) * LANES, cols]
        if key_off is not None and k_lo == jq * LANES:
            blk = jnp.where(tri, blk, NEG)
        return blk

    if mx_ref is not None:
        assert key_off is None
        m_cur = mx_ref[0:1, cols]
    else:
        m_cur = None
        for kb in range(nkb):
            blk = load(kb)
            if blk is not None:
                mk = jnp.max(blk, axis=0, keepdims=True)
                m_cur = mk if m_cur is None else jnp.maximum(m_cur, mk)
    if first:
        m_next = m_cur
    else:
        m_prev = m_ref[0:1, cols]
        m_next = jnp.maximum(m_prev, m_cur)
        a_ref[0:1, cols] = jnp.exp2(m_prev - m_next)
    m_ref[0:1, cols] = m_next
    for kb in range(nkb):
        blk = load(kb)
        rows = slice(kb * LANES, (kb + 1) * LANES)
        if blk is None:
            p_ref[rows, cols] = jnp.zeros((LANES, LANES), BF16)
        else:
            p_ref[rows, cols] = jnp.exp2(blk - m_next).astype(BF16)


def _attn_kernel(q_ref, ka_ref, vt_ref, kma_ref, vmt_ref, o_ref, s_sc, p_sc, a_sc, mx_sc, m_sc, acc_sc, *, tq):
    tk = ATTN_TK
    i = pl.program_id(2)
    nb = tq // tk
    lane = lax.broadcasted_iota(jnp.int32, (tq, PAIR), 1)
    q = q_ref[...]
    zero = jnp.zeros_like(q)
    qa = [jnp.concatenate([jnp.where((lane >= HEAD_DIM) == (hh == 1), q, zero),
                           jnp.where(jnp.abs(lane - (3 * hh + 1)) <= 1, 1.0, 0.0).astype(BF16)], axis=1)
          for hh in range(2)]

    def qk(j, slot, col_lo=0, with_max=True):
        k_blk = ka_ref[pl.ds(pl.multiple_of(j * tk, tk), tk), :]
        for hh in range(2):
            for c0 in range(col_lo, tq, MXU_N):
                st = _dot_nt(k_blk, qa[hh][c0:c0 + MXU_N])
                s_sc[hh, slot, :, c0:c0 + MXU_N] = st
                if with_max:
                    mx_sc[hh, slot, 0:1, c0:c0 + MXU_N] = jnp.max(st, axis=0, keepdims=True)

    def step(j, slot, key_off=None):
        vt_blk = vt_ref[j]
        col_lo = 0 if key_off is None else key_off
        for hh in range(2):
            for c0 in range(col_lo, tq, MXU_N):
                for jq in range(c0 // LANES, (c0 + MXU_N) // LANES):
                    _softmax_cols(s_sc.at[hh, slot], p_sc.at[hh, slot], a_sc.at[hh, slot], m_sc.at[hh], jq, tk,
                                  key_off=key_off, mx_ref=None if key_off is not None else mx_sc.at[hh, slot])
                cs = slice(c0, c0 + MXU_N)
                acc_sc[hh, :, cs] = a_sc[hh, slot, 0:1, cs] * acc_sc[hh, :, cs] + _dot(vt_blk, p_sc[hh, slot, :, cs])

    qk(0, 0)

    for hh in range(2):
        s_sc[hh, 1, 0:LANES, :] = _dot_nt(kma_ref[...], qa[hh])
        for jq in range(tq // LANES):
            _softmax_cols(s_sc.at[hh, 1], p_sc.at[hh, 1], a_sc.at[hh, 1], m_sc.at[hh], jq, LANES, first=True)
        acc_sc[hh] = _dot(vmt_ref[...], p_sc[hh, 1, 0:LANES, :])

    def body(t, carry):
        qk(2 * t + 1, 1)
        step(2 * t, 0)
        qk(2 * t + 2, 0)
        step(2 * t + 1, 1)
        return carry

    assert nb % 2 == 0
    lax.fori_loop(0, i * (nb // 2), body, 0)

    for d in range(nb):
        if d + 1 < nb:
            qk(i * nb + d + 1, (d + 1) % 2, col_lo=(d + 1) * tk, with_max=False)
        step(i * nb + d, d % 2, key_off=d * tk)

    sub = lax.broadcasted_iota(jnp.int32, (PAIR, tq), 0)
    o = jnp.where(sub < HEAD_DIM,
                  acc_sc[0, 0:PAIR, :] / acc_sc[0, PAIR:PAIR + 1, :],
                  acc_sc[1, 0:PAIR, :] / acc_sc[1, PAIR:PAIR + 1, :])
    o_ref[...] = o.T.astype(o_ref.dtype)


def _attn(q, ka, vt, kma, vmt):
    b, s, _ = q.shape
    tk = ATTN_TK
    tq = _tile(s, 2 * tk, 2 * tk)
    assert tq % (2 * tk) == 0 and s % tq == 0, (s, tq)
    kernel = functools.partial(_attn_kernel, tq=tq)
    return pl.pallas_call(
        kernel,
        grid=(b, N_PAIRS, s // tq),
        in_specs=[
            pl.BlockSpec((None, tq, PAIR), lambda bi, p, i: (bi, i, p)),
            pl.BlockSpec((None, None, s, KA), lambda bi, p, i: (p, bi, 0, 0)),
            pl.BlockSpec((None, None, s // tk, V_ROWS, tk), lambda bi, p, i: (p, bi, 0, 0, 0)),
            pl.BlockSpec((None, LANES, KA), lambda bi, p, i: (p, 0, 0)),
            pl.BlockSpec((None, V_ROWS, LANES), lambda bi, p, i: (p, 0, 0)),
        ],
        out_specs=pl.BlockSpec((None, tq, PAIR), lambda bi, p, i: (bi, i, p)),
        out_shape=jax.ShapeDtypeStruct(q.shape, BF16),
        scratch_shapes=[
            pltpu.VMEM((2, 2, tk, tq), F32),
            pltpu.VMEM((2, 2, tk, tq), BF16),
            pltpu.VMEM((2, 2, SUBLANES, tq), F32),
            pltpu.VMEM((2, 2, SUBLANES, tq), F32),
            pltpu.VMEM((2, SUBLANES, tq), F32),
            pltpu.VMEM((2, V_ROWS, tq), F32),
        ],
        compiler_params=_params(("parallel", "parallel", "arbitrary")),
        name="fox_attn",
    )(q, ka, vt, kma, vmt)


def _aug_keys(kb, parts):
    b, n, _ = kb.shape
    k4 = jnp.transpose(kb.reshape(b, n, N_PAIRS, PAIR), (2, 0, 1, 3))
    p4 = jnp.transpose(parts.reshape(b, 3, N_PAIRS, 2, n), (2, 0, 4, 3, 1)).reshape(N_PAIRS, b, n, 6)
    pad = jnp.zeros((N_PAIRS, b, n, KA - PAIR - 6), BF16)
    return jnp.concatenate([k4, p4.astype(BF16), pad], axis=-1)


def _aug_values_t(vb, tk):
    b, n, _ = vb.shape
    v5 = jnp.transpose(vb.reshape(b, n // tk, tk, N_PAIRS, PAIR), (3, 0, 1, 4, 2))
    ones = jnp.ones((N_PAIRS, b, n // tk, 1, tk), BF16)
    pad = jnp.zeros((N_PAIRS, b, n // tk, V_ROWS - PAIR - 1, tk), BF16)
    return jnp.concatenate([v5, ones, pad], axis=3)


def _attn_small_kernel(q_ref, k_ref, v_ref, c_ref, o_ref, *, past):
    t = q_ref.shape[0]
    lk = k_ref.shape[0]
    lane = lax.broadcasted_iota(jnp.int32, (t, PAIR), 1)
    q = q_ref[...]
    zero = jnp.zeros_like(q)
    qpos = past + lax.broadcasted_iota(jnp.int32, (t, lk), 0)
    kpos = lax.broadcasted_iota(jnp.int32, (t, lk), 1)
    outs = []
    for hh in range(2):
        qh = jnp.where((lane >= HEAD_DIM) == (hh == 1), q, zero)
        c = c_ref[hh:hh + 1, :]
        s = _dot_nt(qh, k_ref[...]) + (c[:, past:past + 1] - c)
        s = jnp.where(kpos <= qpos, s, NEG)
        p = jnp.exp2(s - jnp.max(s, axis=1, keepdims=True))
        l = jnp.sum(p, axis=1, keepdims=True)
        outs.append(_dot(p.astype(BF16), v_ref[...]) / l)
    o_ref[...] = jnp.where(lane < HEAD_DIM, outs[0], outs[1]).astype(o_ref.dtype)


def _attn_small(q, kb, vb, c, past):
    b, t, _ = q.shape
    lk = kb.shape[1]
    kernel = functools.partial(_attn_small_kernel, past=past)
    return pl.pallas_call(
        kernel,
        grid=(b, N_PAIRS),
        in_specs=[
            pl.BlockSpec((None, t, PAIR), lambda bi, p: (bi, 0, p)),
            pl.BlockSpec((None, lk, PAIR), lambda bi, p: (bi, 0, p)),
            pl.BlockSpec((None, lk, PAIR), lambda bi, p: (bi, 0, p)),
            pl.BlockSpec((None, None, 2, lk), lambda bi, p: (bi, p, 0, 0)),
        ],
        out_specs=pl.BlockSpec((None, t, PAIR), lambda bi, p: (bi, 0, p)),
        out_shape=jax.ShapeDtypeStruct(q.shape, BF16),
        compiler_params=_params(("parallel", "parallel")),
        name="fox_attn_small",
    )(q, kb, vb, c)


def _layer_norm(x, g, b):
    mu = jnp.mean(x, axis=-1, keepdims=True)
    xc = x - mu
    var = jnp.mean(xc * xc, axis=-1, keepdims=True)
    return xc * lax.rsqrt(var + LN_EPS) * g + b


def _post_kernel(x_ref, att_ref, u_ref, uprev_ref, halo0_ref, g_ref,
                 wgrp_ref, scale_ref, wab_ref, wpb_ref, wout_ref, ln1g_ref, ln1b_ref,
                 wup_ref, wdn_ref, ln2g_ref, ln2b_ref, y_ref, ue_sc, pool_sc, acc_sc,
                 *, alpha, pos0, ff_chunk):
    bb, tm, d_model = x_ref.shape
    rows = bb * tm
    i = pl.program_id(1)

    first = i == 0
    for bi in range(bb):
        ue_sc[bi, 0:HALO, :] = jnp.where(first, halo0_ref[bi], uprev_ref[bi])
        ue_sc[bi, HALO:HALO + tm, :] = u_ref[bi]
    pos = pos0 + i * tm + lax.broadcasted_iota(jnp.int32, (tm, 1), 0)
    for gi, w in enumerate(POOL_WINDOWS):
        ls = slice(gi * POOL_GROUP, (gi + 1) * POOL_GROUP)
        inv_cnt = 1.0 / jnp.minimum(w, pos + 1).astype(F32)
        for bi in range(bb):
            cur = ue_sc[bi, HALO:HALO + tm, ls]
            tot = cur
            for j in range(1, w):
                tot = tot + ue_sc[bi, HALO - j:HALO - j + tm, ls]
            pool_sc[bi * tm:(bi + 1) * tm, ls] = tot * inv_cnt - cur

    mixed = [_dot(pool_sc[:, gi * POOL_GROUP:(gi + 1) * POOL_GROUP].astype(BF16), wgrp_ref[gi])
             for gi in range(len(POOL_WINDOWS))]
    mixed = jnp.concatenate(mixed, axis=1) * scale_ref[...]
    p = _dot(mixed.astype(BF16), wpb_ref[...])
    a = _dot(att_ref[...].reshape(rows, att_ref.shape[2]), wab_ref[...])
    g = g_ref[...].reshape(rows, 2 * d_model).astype(F32)
    m = jax.nn.sigmoid(g[:, :d_model]) * a + jax.nn.sigmoid(g[:, d_model:]) * p
    mix = _dot(m.astype(BF16), wout_ref[...])
    x = x_ref[...].reshape(rows, d_model)
    x1 = _layer_norm(alpha * x + mix, ln1g_ref[...], ln1b_ref[...])

    x1b = x1.astype(BF16)
    d_ff = wup_ref.shape[1]
    for ci, o in enumerate(range(0, d_ff, ff_chunk)):
        h = jnp.maximum(_dot(x1b, wup_ref[:, o:o + ff_chunk]), 0.0)
        part = _dot((h * h).astype(BF16), wdn_ref[o:o + ff_chunk, :])
        if ci == 0:
            acc_sc[...] = part
        else:
            acc_sc[...] += part
    y = _layer_norm(alpha * x1 + acc_sc[...], ln2g_ref[...], ln2b_ref[...])
    y_ref[...] = y.reshape(bb, tm, d_model)


def _post(x, att, u, halo0, g, w, *, alpha, pos0, bb, tm):
    b, t, d_model = x.shape
    nb, nt = b // bb, t // tm
    hb = tm // HALO
    blk = lambda c: pl.BlockSpec((bb, tm, c), lambda bi, i: (bi, i, 0))
    halo0_map = (lambda bi, i: (bi, 0, 0)) if halo0.shape[0] == b else (lambda bi, i: (0, 0, 0))
    if halo0.shape[0] != b:
        assert bb == 1
    weights = [w["w_grp"], w["scale"], w["w_ab"], w["w_pb"], w["w_out"], w["ln1_g"], w["ln1_b"],
               w["w_up"], w["w_dn"], w["ln2_g"], w["ln2_b"]]
    kernel = functools.partial(_post_kernel, alpha=alpha, pos0=pos0, ff_chunk=min(1024, w["w_up"].shape[1]))
    return pl.pallas_call(
        kernel,
        grid=(nb, nt),
        in_specs=[
            blk(d_model), blk(att.shape[2]), blk(u.shape[2]),
            pl.BlockSpec((bb, HALO, u.shape[2]), lambda bi, i: (bi, jnp.maximum(i * hb - 1, 0), 0)),
            pl.BlockSpec((bb, HALO, u.shape[2]), halo0_map),
            blk(2 * d_model),
        ] + [_const_spec(a.shape) for a in weights],
        out_specs=blk(d_model),
        out_shape=jax.ShapeDtypeStruct(x.shape, F32),
        scratch_shapes=[
            pltpu.VMEM((bb, HALO + tm, u.shape[2]), F32),
            pltpu.VMEM((bb * tm, u.shape[2]), F32),
            pltpu.VMEM((bb * tm, d_model), F32),
        ],
        compiler_params=_params(("parallel", "arbitrary")),
        name="post_mix_mlp",
    )(x, att, u, u, halo0, g, *weights)


def _layer_weights(l, w_in, b_f, w_pool_grp, pool_scale, w_attn_br, w_pool_br, w_out,
                   ln1_g, ln1_b, w_up, w_down, ln2_g, ln2_b):
    d_model = w_in.shape[1]
    wi = w_in[l]
    o_f = 3 * D_ATTN
    o_u = o_f + N_HEADS
    pad = jnp.zeros((d_model, LANES - N_HEADS), wi.dtype)
    w_cat = jnp.concatenate([wi[:, :o_f], wi[:, o_u:], wi[:, o_f:o_u], pad], axis=1).astype(BF16)
    bf = jnp.zeros((1, LANES), F32).at[0, :N_HEADS].set(b_f[l].astype(F32))
    row = lambda a: a[l].astype(F32).reshape(1, -1)
    return dict(
        w_cat=w_cat, b_f=bf,
        w_grp=w_pool_grp[l].astype(BF16), scale=row(pool_scale),
        w_ab=w_attn_br[l].astype(BF16), w_pb=w_pool_br[l].astype(BF16), w_out=w_out[l].astype(BF16),
        ln1_g=row(ln1_g), ln1_b=row(ln1_b), w_up=w_up[l].astype(BF16), w_dn=w_down[l].astype(BF16),
        ln2_g=row(ln2_g), ln2_b=row(ln2_b),
    )


def _pad_rows(a, n):
    return jnp.pad(a, ((0, 0), (0, n - a.shape[1]), (0, 0)))


def _c_rows(lf, n, origin=None):
    b, t, _ = lf.shape
    x = jnp.swapaxes(_pad_rows(lf, n), 1, 2)
    c, parts = _cumsum(x, origin=origin, n_valid=t)
    return c.reshape(b, N_PAIRS, 2, n), parts


def _round_up(n, m):
    return -(-n // m) * m


def kernel(x_prompt, x_sample, cache_k, cache_v, cache_logf, state_pool, meta_tokens, w_in, b_f, w_pool_grp,
           pool_scale, w_attn_br, w_pool_br, w_out, ln1_g, ln1_b, w_up, w_down, ln2_g, ln2_b):
    depth = w_in.shape[0]
    b, seq, d_model = x_prompt.shape
    bs, t_dec, _ = x_sample.shape
    past = cache_k.shape[2]
    n_meta = meta_tokens.shape[0]
    assert n_meta == HALO and t_dec >= POOL_STATE and seq >= POOL_STATE
    assert cache_k.shape[3] == N_HEADS and cache_k.shape[4] == HEAD_DIM
    alpha = (2.0 * depth) ** 0.25
    lk_dec = _round_up(past + t_dec, LANES)

    xm = meta_tokens.astype(F32)[None]
    xr = x_prompt
    xs = x_sample
    tm_r = _tile(seq, 512)
    outs = {n: [] for n in ("kp", "vp", "fp", "pp", "ks", "vs", "fs", "ps")}
    for l in range(depth):
        w = _layer_weights(l, w_in, b_f, w_pool_grp, pool_scale, w_attn_br, w_pool_br, w_out,
                           ln1_g, ln1_b, w_up, w_down, ln2_g, ln2_b)

        qm, km, vm, kbm, vbm, um, gm, lfm = _in_proj(xm[0], w["w_cat"], w["b_f"])
        cm, cm_parts = _c_rows(lfm[None], LANES, origin=n_meta - 1)
        kbm_p, vbm_p = _pad_rows(kbm[None], LANES), _pad_rows(vbm[None], LANES)
        attm = _attn_small(qm[None], kbm_p, vbm_p, cm, 0)
        xm = _post(xm, attm, um[None], jnp.zeros((1, HALO, D_POOL), F32), gm[None], w,
                   alpha=alpha, pos0=0, bb=1, tm=n_meta)

        q, k, v, kb, vb, u, g, lf = _in_proj(xr.reshape(b * seq, d_model), w["w_cat"], w["b_f"])
        r3 = lambda a: a.reshape(b, seq, a.shape[-1])
        _, c_parts = _c_rows(r3(lf), seq)
        att = _attn(r3(q), _aug_keys(r3(kb), c_parts), _aug_values_t(r3(vb), ATTN_TK),
                    _aug_keys(kbm_p, cm_parts)[:, 0], _aug_values_t(vbm_p, LANES)[:, 0, 0])
        xr = _post(xr, att, r3(u), um[None], r3(g), w, alpha=alpha, pos0=n_meta, bb=1, tm=tm_r)
        bc = lambda a: jnp.broadcast_to(a[None], (b,) + a.shape)
        outs["kp"].append(jnp.concatenate([bc(km), r3(k)], axis=1).reshape(b, n_meta + seq, N_HEADS, HEAD_DIM))
        outs["vp"].append(jnp.concatenate([bc(vm), r3(v)], axis=1).reshape(b, n_meta + seq, N_HEADS, HEAD_DIM))
        outs["fp"].append(jnp.concatenate([bc(lfm), r3(lf)], axis=1))
        outs["pp"].append(r3(u)[:, seq - POOL_STATE:])

        qs, ks, vs, kbs, vbs, us, gs, lfs = _in_proj(xs.reshape(bs * t_dec, d_model), w["w_cat"], w["b_f"])
        s3 = lambda a: a.reshape(bs, t_dec, a.shape[-1])
        ck = cache_k[l].reshape(bs, past, D_ATTN).astype(BF16)
        cv = cache_v[l].reshape(bs, past, D_ATTN).astype(BF16)
        kfull = _pad_rows(jnp.concatenate([ck, s3(kbs)], axis=1), lk_dec)
        vfull = _pad_rows(jnp.concatenate([cv, s3(vbs)], axis=1), lk_dec)
        cs, _ = _c_rows(jnp.concatenate([cache_logf[l].astype(F32), s3(lfs)], axis=1), lk_dec)
        atts = _attn_small(s3(qs), kfull, vfull, cs, past)
        hist = jnp.concatenate([jnp.zeros((bs, HALO - POOL_STATE, D_POOL), F32), state_pool[l].astype(F32)], axis=1)
        xs = _post(xs, atts, s3(us), hist, s3(gs), w, alpha=alpha, pos0=past, bb=bs, tm=t_dec)
        outs["ks"].append(s3(ks).reshape(bs, t_dec, N_HEADS, HEAD_DIM))
        outs["vs"].append(s3(vs).reshape(bs, t_dec, N_HEADS, HEAD_DIM))
        outs["fs"].append(s3(lfs))
        outs["ps"].append(s3(us)[:, t_dec - POOL_STATE:])

    st = {n: jnp.stack(v) for n, v in outs.items()}
    return (xr, xs, st["kp"], st["vp"], st["fp"], st["pp"], st["ks"], st["vs"], st["fs"], st["ps"])
```

```python
import functools
import math

import jax
import jax.numpy as jnp
from jax import lax
from jax.experimental import pallas as pl
from jax.experimental.pallas import tpu as pltpu

F32 = jnp.float32
BF16 = jnp.bfloat16

N_HEADS = 8
HEAD_DIM = 64
PAIR = 2 * HEAD_DIM
N_PAIRS = N_HEADS // 2
POOL_WINDOWS = (2, 4, 8, 16)
POOL_GROUP = 128
HALO = 16
POOL_STATE = 15
LN_EPS = 1e-5
NEG = -1e30
LANES = 128
SUBLANES = 8
VMEM_LIMIT = 56 * 1024 * 1024


def _dot(a, b):
    return jnp.dot(a, b, preferred_element_type=F32)


def _dot_nt(a, b):
    return lax.dot_general(a, b, (((1,), (1,)), ((), ())), preferred_element_type=F32)


def _tile(n, pref, mult=16):
    best = None
    for t in range(mult, min(n, pref) + 1, mult):
        if n % t == 0:
            best = t
    return best if best is not None else n


def _params(sem):
    return pltpu.CompilerParams(dimension_semantics=sem, vmem_limit_bytes=VMEM_LIMIT)


def _const_spec(shape):
    nd = len(shape)
    return pl.BlockSpec(shape, lambda *_: (0,) * nd, pipeline_mode=pl.Buffered(1))


D_ATTN = N_HEADS * HEAD_DIM
D_POOL = 4 * POOL_GROUP
LOG2E = math.log2(math.e)
Q_SCALE = LOG2E / math.sqrt(HEAD_DIM)


def _in_proj_kernel(x_ref, w_ref, bf_ref, q_ref, k_ref, v_ref, kb_ref, vb_ref, u_ref, g_ref, lf_ref, *, attn_layout):
    d_model = x_ref.shape[1]
    tm = x_ref.shape[0]
    x = x_ref[...].astype(BF16)
    o = 0
    q_ref[...] = (_dot(x, w_ref[:, o:o + D_ATTN]) * Q_SCALE).astype(BF16)
    o += D_ATTN
    k = _dot(x, w_ref[:, o:o + D_ATTN])
    k_ref[...] = k
    o += D_ATTN
    v = _dot(x, w_ref[:, o:o + D_ATTN])
    v_ref[...] = v
    if attn_layout:
        ones_row = jnp.where(lax.broadcasted_iota(jnp.int32, (V_ROWS - PAIR, tm), 0) == 0, 1.0, 0.0).astype(BF16)
        for p in range(N_PAIRS):
            kb_ref[p] = k[:, p * PAIR:(p + 1) * PAIR].astype(BF16)
            vb_ref[p, 0:PAIR, :] = v[:, p * PAIR:(p + 1) * PAIR].T.astype(BF16)
            vb_ref[p, PAIR:V_ROWS, :] = ones_row
    else:
        kb_ref[...] = k.astype(BF16)
        vb_ref[...] = v.astype(BF16)
    o += D_ATTN
    u_ref[...] = _dot(x, w_ref[:, o:o + D_POOL])
    o += D_POOL
    for c in range(2):
        g_ref[:, c * d_model:(c + 1) * d_model] = _dot(x, w_ref[:, o:o + d_model]).astype(BF16)
        o += d_model
    z = _dot(x, w_ref[:, o:o + LANES]) + bf_ref[...]
    lf = jnp.minimum(z, 0.0) - jnp.log1p(jnp.exp(-jnp.abs(z)))
    lf_ref[...] = lf[:, :N_HEADS]


def _in_proj(x, w_cat, b_f, attn_layout=False):
    rows, d_model = x.shape
    tm = _tile(rows, 512)
    n_cols = w_cat.shape[1]
    row = lambda i: (i, 0)
    outs = [
        jax.ShapeDtypeStruct((rows, D_ATTN), BF16),
        jax.ShapeDtypeStruct((rows, D_ATTN), F32),
        jax.ShapeDtypeStruct((rows, D_ATTN), F32),
        jax.ShapeDtypeStruct((rows, D_ATTN), BF16),
        jax.ShapeDtypeStruct((rows, D_ATTN), BF16),
        jax.ShapeDtypeStruct((rows, D_POOL), F32),
        jax.ShapeDtypeStruct((rows, 2 * d_model), BF16),
        jax.ShapeDtypeStruct((rows, N_HEADS), F32),
    ]
    out_specs = [pl.BlockSpec((tm, s.shape[1]), row) for s in outs]
    if attn_layout:
        assert tm == ATTN_TK, (rows, tm)
        outs[3] = jax.ShapeDtypeStruct((N_PAIRS, rows, PAIR), BF16)
        out_specs[3] = pl.BlockSpec((N_PAIRS, tm, PAIR), lambda i: (0, i, 0))
        outs[4] = jax.ShapeDtypeStruct((N_PAIRS, rows // tm, V_ROWS, tm), BF16)
        out_specs[4] = pl.BlockSpec((N_PAIRS, None, V_ROWS, tm), lambda i: (0, i, 0, 0))
    return pl.pallas_call(
        functools.partial(_in_proj_kernel, attn_layout=attn_layout),
        grid=(rows // tm,),
        in_specs=[pl.BlockSpec((tm, d_model), row), _const_spec((d_model, n_cols)), _const_spec((1, LANES))],
        out_specs=out_specs,
        out_shape=outs,
        compiler_params=_params(("parallel",)),
        name="in_proj",
    )(x, w_cat, b_f)


def _cumsum_kernel(x_ref, c_ref, parts_ref, *, origin, n_valid):
    x = x_ref[...]
    n = x.shape[1]
    idx = lax.broadcasted_iota(jnp.int32, x.shape, 1)
    s = 1
    while s < n:
        x = x + jnp.where(idx >= s, pltpu.roll(x, s, 1), 0.0)
        s *= 2
    c = x * LOG2E
    c_ref[...] = c
    bias = -c if origin is None else c[:, origin:origin + 1] - c
    if n_valid < n:
        bias = jnp.where(idx < n_valid, bias, NEG)
    hi = bias.astype(BF16).astype(F32)
    r = bias - hi
    mid = r.astype(BF16).astype(F32)
    parts_ref[0] = hi
    parts_ref[1] = mid
    parts_ref[2] = (r - mid).astype(BF16).astype(F32)


def _cumsum(x, origin=None, n_valid=None):
    b, h, n = x.shape
    spec = pl.BlockSpec((None, h, n), lambda i: (i, 0, 0))
    kernel = functools.partial(_cumsum_kernel, origin=origin, n_valid=n if n_valid is None else n_valid)
    return pl.pallas_call(
        kernel, grid=(b,), in_specs=[spec],
        out_specs=[spec, pl.BlockSpec((None, 3, h, n), lambda i: (i, 0, 0, 0))],
        out_shape=[jax.ShapeDtypeStruct(x.shape, F32), jax.ShapeDtypeStruct((b, 3, h, n), F32)],
        compiler_params=_params(("parallel",)), name="logf_cumsum",
    )(x)


ATTN_TK = 512
KA = 2 * LANES
V_ROWS = PAIR + 16
MXU_N = 256


def _softmax_cols(s_ref, p_ref, a_ref, m_ref, jq, nkeys, first=False, key_off=None, mx_ref=None):
    cols = slice(jq * LANES, (jq + 1) * LANES)
    nkb = nkeys // LANES
    if key_off is not None:
        tri = (lax.broadcasted_iota(jnp.int32, (LANES, LANES), 0)
               <= lax.broadcasted_iota(jnp.int32, (LANES, LANES), 1))

    def load(kb):
        if key_off is not None:
            k_lo = key_off + kb * LANES
            if k_lo > jq * LANES:
                return None
        blk = s_ref[kb * LANES:(kb + 1) * LANES, cols]
        if key_off is not None and k_lo == jq * LANES:
            blk = jnp.where(tri, blk, NEG)
        return blk

    if mx_ref is not None:
        assert key_off is None
        m_cur = mx_ref[0:1, cols]
    else:
        m_cur = None
        for kb in range(nkb):
            blk = load(kb)
            if blk is not None:
                mk = jnp.max(blk, axis=0, keepdims=True)
                m_cur = mk if m_cur is None else jnp.maximum(m_cur, mk)
    if first:
        m_next = m_cur
    else:
        m_prev = m_ref[0:1, cols]
        m_next = jnp.maximum(m_prev, m_cur)
        a_ref[0:1, cols] = jnp.exp2(m_prev - m_next)
    m_ref[0:1, cols] = m_next
    for kb in range(nkb):
        blk = load(kb)
        rows = slice(kb * LANES, (kb + 1) * LANES)
        if blk is None:
            p_ref[rows, cols] = jnp.zeros((LANES, LANES), BF16)
        else:
            p_ref[rows, cols] = jnp.exp2(blk - m_next).astype(BF16)


def _attn_kernel(q_ref, ka_ref, vt_ref, kma_ref, vmt_ref, o_ref, s_sc, p_sc, a_sc, mx_sc, m_sc, acc_sc, *, tq):
    tk = ATTN_TK
    i = pl.program_id(2)
    nb = tq // tk
    lane = lax.broadcasted_iota(jnp.int32, (tq, PAIR), 1)
    q = q_ref[...]
    zero = jnp.zeros_like(q)
    qa = [jnp.concatenate([jnp.where((lane >= HEAD_DIM) == (hh == 1), q, zero),
                           jnp.where(jnp.abs(lane - (3 * hh + 1)) <= 1, 1.0, 0.0).astype(BF16)], axis=1)
          for hh in range(2)]

    def qk(j, slot, col_lo=0, with_max=True):
        k_blk = ka_ref[pl.ds(pl.multiple_of(j * tk, tk), tk), :]
        for hh in range(2):
            for c0 in range(col_lo, tq, MXU_N):
                st = _dot_nt(k_blk, qa[hh][c0:c0 + MXU_N])
                s_sc[hh, slot, :, c0:c0 + MXU_N] = st
                if with_max:
                    mx_sc[hh, slot, 0:1, c0:c0 + MXU_N] = jnp.max(st, axis=0, keepdims=True)

    def step(j, slot, key_off=None):
        vt_blk = vt_ref[j]
        col_lo = 0 if key_off is None else key_off
        for hh in range(2):
            for c0 in range(col_lo, tq, MXU_N):
                for jq in range(c0 // LANES, (c0 + MXU_N) // LANES):
                    _softmax_cols(s_sc.at[hh, slot], p_sc.at[hh, slot], a_sc.at[hh, slot], m_sc.at[hh], jq, tk,
                                  key_off=key_off, mx_ref=None if key_off is not None else mx_sc.at[hh, slot])
                cs = slice(c0, c0 + MXU_N)
                acc_sc[hh, :, cs] = a_sc[hh, slot, 0:1, cs] * acc_sc[hh, :, cs] + _dot(vt_blk, p_sc[hh, slot, :, cs])

    qk(0, 0)

    for hh in range(2):
        s_sc[hh, 1, 0:LANES, :] = _dot_nt(kma_ref[...], qa[hh])
        for jq in range(tq // LANES):
            _softmax_cols(s_sc.at[hh, 1], p_sc.at[hh, 1], a_sc.at[hh, 1], m_sc.at[hh], jq, LANES, first=True)
        acc_sc[hh] = _dot(vmt_ref[...], p_sc[hh, 1, 0:LANES, :])

    def body(t, carry):
        qk(2 * t + 1, 1)
        step(2 * t, 0)
        qk(2 * t + 2, 0)
        step(2 * t + 1, 1)
        return carry

    assert nb % 2 == 0
    lax.fori_loop(0, i * (nb // 2), body, 0)

    for d in range(nb):
        if d + 1 < nb:
            qk(i * nb + d + 1, (d + 1) % 2, col_lo=(d + 1) * tk, with_max=False)
        step(i * nb + d, d % 2, key_off=d * tk)

    sub = lax.broadcasted_iota(jnp.int32, (PAIR, tq), 0)
    o = jnp.where(sub < HEAD_DIM,
                  acc_sc[0, 0:PAIR, :] / acc_sc[0, PAIR:PAIR + 1, :],
                  acc_sc[1, 0:PAIR, :] / acc_sc[1, PAIR:PAIR + 1, :])
    o_ref[...] = o.T.astype(o_ref.dtype)


def _attn(q, ka, vt, kma, vmt):
    b, s, _ = q.shape
    tk = ATTN_TK
    tq = _tile(s, 2 * tk, 2 * tk)
    assert tq % (2 * tk) == 0 and s % tq == 0, (s, tq)
    kernel = functools.partial(_attn_kernel, tq=tq)
    return pl.pallas_call(
        kernel,
        grid=(b, N_PAIRS, s // tq),
        in_specs=[
            pl.BlockSpec((None, tq, PAIR), lambda bi, p, i: (bi, i, p)),
            pl.BlockSpec((None, None, s, KA), lambda bi, p, i: (p, bi, 0, 0)),
            pl.BlockSpec((None, None, s // tk, V_ROWS, tk), lambda bi, p, i: (p, bi, 0, 0, 0)),
            pl.BlockSpec((None, LANES, KA), lambda bi, p, i: (p, 0, 0)),
            pl.BlockSpec((None, V_ROWS, LANES), lambda bi, p, i: (p, 0, 0)),
        ],
        out_specs=pl.BlockSpec((None, tq, PAIR), lambda bi, p, i: (bi, i, p)),
        out_shape=jax.ShapeDtypeStruct(q.shape, BF16),
        scratch_shapes=[
            pltpu.VMEM((2, 2, tk, tq), F32),
            pltpu.VMEM((2, 2, tk, tq), BF16),
            pltpu.VMEM((2, 2, SUBLANES, tq), F32),
            pltpu.VMEM((2, 2, SUBLANES, tq), F32),
            pltpu.VMEM((2, SUBLANES, tq), F32),
            pltpu.VMEM((2, V_ROWS, tq), F32),
        ],
        compiler_params=_params(("parallel", "parallel", "arbitrary")),
        name="fox_attn",
    )(q, ka, vt, kma, vmt)


def _aug_keys(kb, parts):
    b, n, _ = kb.shape
    return _aug_keys4(jnp.transpose(kb.reshape(b, n, N_PAIRS, PAIR), (2, 0, 1, 3)), parts)


def _aug_keys4(k4, parts):
    _, b, n, _ = k4.shape
    p4 = jnp.transpose(parts.reshape(b, 3, N_PAIRS, 2, n), (2, 0, 4, 3, 1)).reshape(N_PAIRS, b, n, 6)
    pad = jnp.zeros((N_PAIRS, b, n, KA - PAIR - 6), BF16)
    return jnp.concatenate([k4, p4.astype(BF16), pad], axis=-1)


def _aug_values_t(vb, tk):
    b, n, _ = vb.shape
    v5 = jnp.transpose(vb.reshape(b, n // tk, tk, N_PAIRS, PAIR), (3, 0, 1, 4, 2))
    ones = jnp.ones((N_PAIRS, b, n // tk, 1, tk), BF16)
    pad = jnp.zeros((N_PAIRS, b, n // tk, V_ROWS - PAIR - 1, tk), BF16)
    return jnp.concatenate([v5, ones, pad], axis=3)


def _attn_small_kernel(q_ref, k_ref, v_ref, c_ref, o_ref, *, past):
    t = q_ref.shape[0]
    lk = k_ref.shape[0]
    lane = lax.broadcasted_iota(jnp.int32, (t, PAIR), 1)
    q = q_ref[...]
    zero = jnp.zeros_like(q)
    qpos = past + lax.broadcasted_iota(jnp.int32, (t, lk), 0)
    kpos = lax.broadcasted_iota(jnp.int32, (t, lk), 1)
    outs = []
    for hh in range(2):
        qh = jnp.where((lane >= HEAD_DIM) == (hh == 1), q, zero)
        c = c_ref[hh:hh + 1, :]
        s = _dot_nt(qh, k_ref[...]) + (c[:, past:past + 1] - c)
        s = jnp.where(kpos <= qpos, s, NEG)
        p = jnp.exp2(s - jnp.max(s, axis=1, keepdims=True))
        l = jnp.sum(p, axis=1, keepdims=True)
        outs.append(_dot(p.astype(BF16), v_ref[...]) / l)
    o_ref[...] = jnp.where(lane < HEAD_DIM, outs[0], outs[1]).astype(o_ref.dtype)


def _attn_small(q, kb, vb, c, past):
    b, t, _ = q.shape
    lk = kb.shape[1]
    kernel = functools.partial(_attn_small_kernel, past=past)
    return pl.pallas_call(
        kernel,
        grid=(b, N_PAIRS),
        in_specs=[
            pl.BlockSpec((None, t, PAIR), lambda bi, p: (bi, 0, p)),
            pl.BlockSpec((None, lk, PAIR), lambda bi, p: (bi, 0, p)),
            pl.BlockSpec((None, lk, PAIR), lambda bi, p: (bi, 0, p)),
            pl.BlockSpec((None, None, 2, lk), lambda bi, p: (bi, p, 0, 0)),
        ],
        out_specs=pl.BlockSpec((None, t, PAIR), lambda bi, p: (bi, 0, p)),
        out_shape=jax.ShapeDtypeStruct(q.shape, BF16),
        compiler_params=_params(("parallel", "parallel")),
        name="fox_attn_small",
    )(q, kb, vb, c)


def _layer_norm(x, g, b):
    mu = jnp.mean(x, axis=-1, keepdims=True)
    xc = x - mu
    var = jnp.mean(xc * xc, axis=-1, keepdims=True)
    return xc * lax.rsqrt(var + LN_EPS) * g + b


def _post_kernel(x_ref, att_ref, u_ref, uprev_ref, halo0_ref, g_ref,
                 wgrp_ref, scale_ref, wab_ref, wpb_ref, wout_ref, ln1g_ref, ln1b_ref,
                 wup_ref, wdn_ref, ln2g_ref, ln2b_ref, y_ref, ue_sc, pool_sc, acc_sc,
                 *, alpha, pos0, ff_chunk):
    bb, tm, d_model = x_ref.shape
    rows = bb * tm
    i = pl.program_id(1)

    first = i == 0
    for bi in range(bb):
        ue_sc[bi, 0:HALO, :] = jnp.where(first, halo0_ref[bi], uprev_ref[bi])
        ue_sc[bi, HALO:HALO + tm, :] = u_ref[bi]
    pos = pos0 + i * tm + lax.broadcasted_iota(jnp.int32, (tm, 1), 0)
    for gi, w in enumerate(POOL_WINDOWS):
        ls = slice(gi * POOL_GROUP, (gi + 1) * POOL_GROUP)
        inv_cnt = 1.0 / jnp.minimum(w, pos + 1).astype(F32)
        for bi in range(bb):
            cur = ue_sc[bi, HALO:HALO + tm, ls]
            tot = cur
            for j in range(1, w):
                tot = tot + ue_sc[bi, HALO - j:HALO - j + tm, ls]
            pool_sc[bi * tm:(bi + 1) * tm, ls] = tot * inv_cnt - cur

    mixed = [_dot(pool_sc[:, gi * POOL_GROUP:(gi + 1) * POOL_GROUP].astype(BF16), wgrp_ref[gi])
             for gi in range(len(POOL_WINDOWS))]
    mixed = jnp.concatenate(mixed, axis=1) * scale_ref[...]
    p = _dot(mixed.astype(BF16), wpb_ref[...])
    a = _dot(att_ref[...].reshape(rows, att_ref.shape[2]), wab_ref[...])
    g = g_ref[...].reshape(rows, 2 * d_model).astype(F32)
    m = jax.nn.sigmoid(g[:, :d_model]) * a + jax.nn.sigmoid(g[:, d_model:]) * p
    mix = _dot(m.astype(BF16), wout_ref[...])
    x = x_ref[...].reshape(rows, d_model)
    x1 = _layer_norm(alpha * x + mix, ln1g_ref[...], ln1b_ref[...])

    x1b = x1.astype(BF16)
    d_ff = wup_ref.shape[1]
    for ci, o in enumerate(range(0, d_ff, ff_chunk)):
        h = jnp.maximum(_dot(x1b, wup_ref[:, o:o + ff_chunk]), 0.0)
        part = _dot((h * h).astype(BF16), wdn_ref[o:o + ff_chunk, :])
        if ci == 0:
            acc_sc[...] = part
        else:
            acc_sc[...] += part
    y = _layer_norm(alpha * x1 + acc_sc[...], ln2g_ref[...], ln2b_ref[...])
    y_ref[...] = y.reshape(bb, tm, d_model)


def _post(x, att, u, halo0, g, w, *, alpha, pos0, bb, tm):
    b, t, d_model = x.shape
    nb, nt = b // bb, t // tm
    hb = tm // HALO
    blk = lambda c: pl.BlockSpec((bb, tm, c), lambda bi, i: (bi, i, 0))
    halo0_map = (lambda bi, i: (bi, 0, 0)) if halo0.shape[0] == b else (lambda bi, i: (0, 0, 0))
    if halo0.shape[0] != b:
        assert bb == 1
    weights = [w["w_grp"], w["scale"], w["w_ab"], w["w_pb"], w["w_out"], w["ln1_g"], w["ln1_b"],
               w["w_up"], w["w_dn"], w["ln2_g"], w["ln2_b"]]
    kernel = functools.partial(_post_kernel, alpha=alpha, pos0=pos0, ff_chunk=min(1024, w["w_up"].shape[1]))
    return pl.pallas_call(
        kernel,
        grid=(nb, nt),
        in_specs=[
            blk(d_model), blk(att.shape[2]), blk(u.shape[2]),
            pl.BlockSpec((bb, HALO, u.shape[2]), lambda bi, i: (bi, jnp.maximum(i * hb - 1, 0), 0)),
            pl.BlockSpec((bb, HALO, u.shape[2]), halo0_map),
            blk(2 * d_model),
        ] + [_const_spec(a.shape) for a in weights],
        out_specs=blk(d_model),
        out_shape=jax.ShapeDtypeStruct(x.shape, F32),
        scratch_shapes=[
            pltpu.VMEM((bb, HALO + tm, u.shape[2]), F32),
            pltpu.VMEM((bb * tm, u.shape[2]), F32),
            pltpu.VMEM((bb * tm, d_model), F32),
        ],
        compiler_params=_params(("parallel", "arbitrary")),
        name="post_mix_mlp",
    )(x, att, u, u, halo0, g, *weights)


def _layer_weights(l, w_in, b_f, w_pool_grp, pool_scale, w_attn_br, w_pool_br, w_out,
                   ln1_g, ln1_b, w_up, w_down, ln2_g, ln2_b):
    d_model = w_in.shape[1]
    wi = w_in[l]
    o_f = 3 * D_ATTN
    o_u = o_f + N_HEADS
    pad = jnp.zeros((d_model, LANES - N_HEADS), wi.dtype)
    w_cat = jnp.concatenate([wi[:, :o_f], wi[:, o_u:], wi[:, o_f:o_u], pad], axis=1).astype(BF16)
    bf = jnp.zeros((1, LANES), F32).at[0, :N_HEADS].set(b_f[l].astype(F32))
    row = lambda a: a[l].astype(F32).reshape(1, -1)
    return dict(
        w_cat=w_cat, b_f=bf,
        w_grp=w_pool_grp[l].astype(BF16), scale=row(pool_scale),
        w_ab=w_attn_br[l].astype(BF16), w_pb=w_pool_br[l].astype(BF16), w_out=w_out[l].astype(BF16),
        ln1_g=row(ln1_g), ln1_b=row(ln1_b), w_up=w_up[l].astype(BF16), w_dn=w_down[l].astype(BF16),
        ln2_g=row(ln2_g), ln2_b=row(ln2_b),
    )


def _pad_rows(a, n):
    return jnp.pad(a, ((0, 0), (0, n - a.shape[1]), (0, 0)))


def _c_rows(lf, n, origin=None):
    b, t, _ = lf.shape
    x = jnp.swapaxes(_pad_rows(lf, n), 1, 2)
    c, parts = _cumsum(x, origin=origin, n_valid=t)
    return c.reshape(b, N_PAIRS, 2, n), parts


def _round_up(n, m):
    return -(-n // m) * m


def kernel(x_prompt, x_sample, cache_k, cache_v, cache_logf, state_pool, meta_tokens, w_in, b_f, w_pool_grp,
           pool_scale, w_attn_br, w_pool_br, w_out, ln1_g, ln1_b, w_up, w_down, ln2_g, ln2_b):
    depth = w_in.shape[0]
    b, seq, d_model = x_prompt.shape
    bs, t_dec, _ = x_sample.shape
    past = cache_k.shape[2]
    n_meta = meta_tokens.shape[0]
    assert n_meta == HALO and t_dec >= POOL_STATE and seq >= POOL_STATE
    assert cache_k.shape[3] == N_HEADS and cache_k.shape[4] == HEAD_DIM
    alpha = (2.0 * depth) ** 0.25
    lk_dec = _round_up(past + t_dec, LANES)

    xm = meta_tokens.astype(F32)[None]
    xr = x_prompt
    xs = x_sample
    tm_r = _tile(seq, 512)
    outs = {n: [] for n in ("kp", "vp", "fp", "pp", "ks", "vs", "fs", "ps")}
    for l in range(depth):
        w = _layer_weights(l, w_in, b_f, w_pool_grp, pool_scale, w_attn_br, w_pool_br, w_out,
                           ln1_g, ln1_b, w_up, w_down, ln2_g, ln2_b)

        qm, km, vm, kbm, vbm, um, gm, lfm = _in_proj(xm[0], w["w_cat"], w["b_f"])
        cm, cm_parts = _c_rows(lfm[None], LANES, origin=n_meta - 1)
        kbm_p, vbm_p = _pad_rows(kbm[None], LANES), _pad_rows(vbm[None], LANES)
        attm = _attn_small(qm[None], kbm_p, vbm_p, cm, 0)
        xm = _post(xm, attm, um[None], jnp.zeros((1, HALO, D_POOL), F32), gm[None], w,
                   alpha=alpha, pos0=0, bb=1, tm=n_meta)

        q, k, v, kb4, vt, u, g, lf = _in_proj(xr.reshape(b * seq, d_model), w["w_cat"], w["b_f"], attn_layout=True)
        r3 = lambda a: a.reshape(b, seq, a.shape[-1])
        _, c_parts = _c_rows(r3(lf), seq)
        att = _attn(r3(q), _aug_keys4(kb4.reshape(N_PAIRS, b, seq, PAIR), c_parts),
                    vt.reshape(N_PAIRS, b, seq // ATTN_TK, V_ROWS, ATTN_TK),
                    _aug_keys(kbm_p, cm_parts)[:, 0], _aug_values_t(vbm_p, LANES)[:, 0, 0])
        xr = _post(xr, att, r3(u), um[None], r3(g), w, alpha=alpha, pos0=n_meta, bb=1, tm=tm_r)
        bc = lambda a: jnp.broadcast_to(a[None], (b,) + a.shape)
        outs["kp"].append(jnp.concatenate([bc(km), r3(k)], axis=1).reshape(b, n_meta + seq, N_HEADS, HEAD_DIM))
        outs["vp"].append(jnp.concatenate([bc(vm), r3(v)], axis=1).reshape(b, n_meta + seq, N_HEADS, HEAD_DIM))
        outs["fp"].append(jnp.concatenate([bc(lfm), r3(lf)], axis=1))
        outs["pp"].append(r3(u)[:, seq - POOL_STATE:])

        qs, ks, vs, kbs, vbs, us, gs, lfs = _in_proj(xs.reshape(bs * t_dec, d_model), w["w_cat"], w["b_f"])
        s3 = lambda a: a.reshape(bs, t_dec, a.shape[-1])
        ck = cache_k[l].reshape(bs, past, D_ATTN).astype(BF16)
        cv = cache_v[l].reshape(bs, past, D_ATTN).astype(BF16)
        kfull = _pad_rows(jnp.concatenate([ck, s3(kbs)], axis=1), lk_dec)
        vfull = _pad_rows(jnp.concatenate([cv, s3(vbs)], axis=1), lk_dec)
        cs, _ = _c_rows(jnp.concatenate([cache_logf[l].astype(F32), s3(lfs)], axis=1), lk_dec)
        atts = _attn_small(s3(qs), kfull, vfull, cs, past)
        hist = jnp.concatenate([jnp.zeros((bs, HALO - POOL_STATE, D_POOL), F32), state_pool[l].astype(F32)], axis=1)
        xs = _post(xs, atts, s3(us), hist, s3(gs), w, alpha=alpha, pos0=past, bb=bs, tm=t_dec)
        outs["ks"].append(s3(ks).reshape(bs, t_dec, N_HEADS, HEAD_DIM))
        outs["vs"].append(s3(vs).reshape(bs, t_dec, N_HEADS, HEAD_DIM))
        outs["fs"].append(s3(lfs))
        outs["ps"].append(s3(us)[:, t_dec - POOL_STATE:])

    st = {n: jnp.stack(v) for n, v in outs.items()}
    return (xr, xs, st["kp"], st["vp"], st["fp"], st["pp"], st["ks"], st["vs"], st["fs"], st["ps"])
```
